```python
import math
import jax, jax.numpy as jnp
from jax import lax
import numpy as np

D_MODEL = 1024
BATCH = 16
SEQ = 4096
DEPTH = 2
DEC_BATCH = 32
DEC_SEQ = 16
PAST_LEN = 4096

CHUNK = 64
N_MIXERS = 2
N_CONV_LAYERS = (DEPTH + 1) // 2
N_FOX_LAYERS = DEPTH // 2
E_CONV = D_MODEL
CONV_WIDTH = 31
CONV_STATE = CONV_WIDTH - 1
N_HEADS = 16
HEAD_DIM = D_MODEL // N_HEADS
E_FOX = N_HEADS * HEAD_DIM
Q_BLOCK = 128
FORGET_BIAS_LO = 2.0
FORGET_BIAS_HI = 5.0
EPS = 1e-6

kernel_name = 'streaming_conv_fox_hybrid'


def rmsnorm(x, g):
    xf = x.astype(jnp.float32)
    y = xf * lax.rsqrt(jnp.mean(xf * xf, axis=-1, keepdims=True) + EPS)
    return y.astype(x.dtype) * g


def layernorm(x, g, b):
    xf = x.astype(jnp.float32)
    mu = jnp.mean(xf, axis=-1, keepdims=True)
    xc = xf - mu
    y = xc * lax.rsqrt(jnp.mean(xc * xc, axis=-1, keepdims=True) + EPS)
    return y.astype(x.dtype) * g + b


def conv_mixer(h, buf, w_in, w_dw, b_dw, ln_g, ln_b, w_out):
    u = jnp.einsum('btd,de->bte', h, w_in)
    a, g_glu, z = jnp.split(u, 3, axis=-1)
    v = a * jax.nn.sigmoid(g_glu)
    vp = jnp.concatenate([buf.astype(v.dtype), v], axis=1)
    y = lax.conv_general_dilated(
        vp, w_dw[:, None, :].astype(vp.dtype), window_strides=(1,), padding='VALID',
        dimension_numbers=('NWC', 'WIO', 'NWC'), feature_group_count=E_CONV) + b_dw
    y = jax.nn.silu(layernorm(y, ln_g, ln_b))
    out = jnp.einsum('bte,ed->btd', y * jax.nn.silu(z), w_out)
    return out, vp[:, -CONV_STATE:]


def fox_mixer(h, past_k, past_v, past_logf, w_in, b_f, qn_g, kn_g, w_out):
    B, T, _ = h.shape
    P = past_k.shape[1]
    u = jnp.einsum('btd,de->bte', h, w_in)
    q, k, v, z, fl = jnp.split(u, [E_FOX, 2 * E_FOX, 3 * E_FOX, 4 * E_FOX], axis=-1)
    q = rmsnorm(q.reshape(B, T, N_HEADS, HEAD_DIM), qn_g)
    k = rmsnorm(k.reshape(B, T, N_HEADS, HEAD_DIM), kn_g)
    v = v.reshape(B, T, N_HEADS, HEAD_DIM)
    logf = jax.nn.log_sigmoid((fl + b_f).astype(jnp.float32))
    k_all = jnp.concatenate([past_k.astype(k.dtype), k], axis=1)
    v_all = jnp.concatenate([past_v.astype(v.dtype), v], axis=1)
    c = jnp.cumsum(jnp.concatenate([past_logf.astype(jnp.float32), logf], axis=1), axis=1)
    scale = 1.0 / math.sqrt(HEAD_DIM)
    outs = []
    for start in range(0, T, Q_BLOCK):
        end = min(start + Q_BLOCK, T)
        nk = P + end
        s = jnp.einsum('bqhd,bkhd->bhqk', q[:, start:end], k_all[:, :nk]).astype(jnp.float32) * scale
        cq = jnp.transpose(c[:, P + start:P + end], (0, 2, 1))[:, :, :, None]
        ck = jnp.transpose(c[:, :nk], (0, 2, 1))[:, :, None, :]
        s = s + (cq - ck)
        q_pos = P + start + jnp.arange(end - start)
        k_pos = jnp.arange(nk)
        mask = k_pos[None, :] <= q_pos[:, None]
        s = jnp.where(mask[None, None], s, -jnp.inf)
        p = jax.nn.softmax(s, axis=-1).astype(v_all.dtype)
        outs.append(jnp.einsum('bhqk,bkhd->bqhd', p, v_all[:, :nk]))
    o = jnp.concatenate(outs, axis=1).reshape(B, T, E_FOX)
    out = jnp.einsum('bte,ed->btd', o * jax.nn.silu(z), w_out)
    return out, k, v, logf


def setup_inputs(seed: int = 0) -> dict:
    key = jax.random.key(seed)
    ks = jax.random.split(key, 20)
    nrm = jax.random.normal
    return {
        'x_prompt': nrm(ks[0], (BATCH, SEQ, D_MODEL), jnp.float32),
        'x_sample': nrm(ks[1], (DEC_BATCH, DEC_SEQ, D_MODEL), jnp.float32),
        'state_conv': 0.5 * nrm(ks[2], (N_CONV_LAYERS, DEC_BATCH, CONV_STATE, E_CONV), jnp.float32),
        'cache_k': nrm(ks[3], (N_FOX_LAYERS, DEC_BATCH, PAST_LEN, N_HEADS, HEAD_DIM), jnp.float32),
        'cache_v': nrm(ks[4], (N_FOX_LAYERS, DEC_BATCH, PAST_LEN, N_HEADS, HEAD_DIM), jnp.float32),
        'cache_logf': jax.nn.log_sigmoid(3.0 + nrm(ks[5], (N_FOX_LAYERS, DEC_BATCH, PAST_LEN, N_HEADS), jnp.float32)),
        'norm_g': 1.0 + 0.05 * nrm(ks[6], (DEPTH, D_MODEL), jnp.float32),
        'final_norm_g': 1.0 + 0.05 * nrm(ks[7], (D_MODEL,), jnp.float32),
        'w_conv_in': nrm(ks[8], (N_CONV_LAYERS, D_MODEL, 3 * E_CONV), jnp.float32) * D_MODEL ** -0.5,
        'w_dw': nrm(ks[9], (N_CONV_LAYERS, CONV_WIDTH, E_CONV), jnp.float32) * CONV_WIDTH ** -0.5,
        'b_dw': 0.02 * nrm(ks[10], (N_CONV_LAYERS, E_CONV), jnp.float32),
        'conv_ln_g': 1.0 + 0.05 * nrm(ks[11], (N_CONV_LAYERS, E_CONV), jnp.float32),
        'conv_ln_b': 0.02 * nrm(ks[12], (N_CONV_LAYERS, E_CONV), jnp.float32),
        'w_conv_out': nrm(ks[13], (N_CONV_LAYERS, E_CONV, D_MODEL), jnp.float32) * E_CONV ** -0.5,
        'w_fox_in': nrm(ks[14], (N_FOX_LAYERS, D_MODEL, 4 * E_FOX + N_HEADS), jnp.float32) * D_MODEL ** -0.5,
        'b_forget': jax.random.uniform(ks[15], (N_FOX_LAYERS, N_HEADS), jnp.float32, FORGET_BIAS_LO, FORGET_BIAS_HI),
        'q_norm_g': 1.0 + 0.05 * nrm(ks[16], (N_FOX_LAYERS, HEAD_DIM), jnp.float32),
        'k_norm_g': 1.0 + 0.05 * nrm(ks[17], (N_FOX_LAYERS, HEAD_DIM), jnp.float32),
        'w_fox_out': nrm(ks[18], (N_FOX_LAYERS, E_FOX, D_MODEL), jnp.float32) * E_FOX ** -0.5,
    }


def reference(x_prompt, x_sample, state_conv, cache_k, cache_v, cache_logf,
              norm_g, final_norm_g, w_conv_in, w_dw, b_dw, conv_ln_g, conv_ln_b, w_conv_out,
              w_fox_in, b_forget, q_norm_g, k_norm_g, w_fox_out):
    yp, ys = x_prompt, x_sample
    Bp = x_prompt.shape[0]
    conv_p, conv_s = [], []
    k_p, v_p, lf_p, k_s, v_s, lf_s = [], [], [], [], [], []
    for i in range(DEPTH):
        hp = rmsnorm(yp, norm_g[i])
        hs = rmsnorm(ys, norm_g[i])
        j = i // N_MIXERS
        if i % N_MIXERS == 0:
            cw = (w_conv_in[j], w_dw[j], b_dw[j], conv_ln_g[j], conv_ln_b[j], w_conv_out[j])
            op, bp = conv_mixer(hp, jnp.zeros((Bp, CONV_STATE, E_CONV), hp.dtype), *cw)
            os_, bs = conv_mixer(hs, state_conv[j], *cw)
            conv_p.append(bp)
            conv_s.append(bs)
        else:
            fw = (w_fox_in[j], b_forget[j], q_norm_g[j], k_norm_g[j], w_fox_out[j])
            op, kp_, vp_, lp_ = fox_mixer(
                hp, jnp.zeros((Bp, 0, N_HEADS, HEAD_DIM), hp.dtype), jnp.zeros((Bp, 0, N_HEADS, HEAD_DIM), hp.dtype),
                jnp.zeros((Bp, 0, N_HEADS), jnp.float32), *fw)
            os_, ks_, vs_, ls_ = fox_mixer(hs, cache_k[j], cache_v[j], cache_logf[j], *fw)
            k_p.append(kp_); v_p.append(vp_); lf_p.append(lp_)
            k_s.append(ks_); v_s.append(vs_); lf_s.append(ls_)
        yp = yp + op
        ys = ys + os_
    y_prompt = rmsnorm(yp, final_norm_g)
    y_sample = rmsnorm(ys, final_norm_g)
    return (y_prompt, y_sample, jnp.stack(conv_p), jnp.stack(conv_s),
            jnp.stack(k_p), jnp.stack(v_p), jnp.stack(lf_p),
            jnp.stack(k_s), jnp.stack(v_s), jnp.stack(lf_s))
```

```python
import functools
import math

import jax
import jax.numpy as jnp
from jax import lax
from jax.experimental import pallas as pl
from jax.experimental.pallas import tpu as pltpu

N_HEADS = 16
HEAD_DIM = 64
CONV_WIDTH = 31
CONV_STATE = CONV_WIDTH - 1
EPS = 1e-6
LOG2E = 1.4426950408889634
NEG_BIG = -1e30

F32 = jnp.float32
BF16 = jnp.bfloat16

CONV_PAD = 32
CONV_ROWS = 32
VMEM_LIMIT = 56 * 1024 * 1024


def _dot(a, b):
    return jnp.dot(a, b, preferred_element_type=F32)


def _dot_nt(a, b):
    return lax.dot_general(a, b, (((1,), (1,)), ((), ())), preferred_element_type=F32)


def _split3(x):
    hi = x.astype(BF16)
    r1 = x - hi.astype(F32)
    mid = r1.astype(BF16)
    lo = (r1 - mid.astype(F32)).astype(BF16)
    return hi, mid, lo


def _log_sigmoid(x):
    return jnp.minimum(x, 0.0) - jnp.log1p(jnp.exp(-jnp.abs(x)))


def _rms(x):
    return x * lax.rsqrt(jnp.mean(x * x, axis=-1, keepdims=True) + EPS)


def _const_spec(shape):
    nd = len(shape)
    return pl.BlockSpec(shape, lambda *_: (0,) * nd)


def _conv_layer_body(x_ref, st_ref, g_ref, win_ref, wdw_ref, bdw_ref, lng_ref, lnb_ref, wout_ref,
                     y_ref, cst_ref, vbuf, z_sc, gated_sc):
    t = pl.program_id(1)
    nt = pl.num_programs(1)
    bb, tq, d = x_ref.shape
    e = wout_ref.shape[0]

    @pl.when(t == 0)
    def _():
        vbuf[:, CONV_PAD - CONV_STATE:CONV_PAD, :] = st_ref[...]

    x = x_ref[...].reshape(bb * tq, d)
    hb = (_rms(x) * g_ref[...]).astype(BF16)
    a = _dot(hb, win_ref[:, 0:e])
    g = _dot(hb, win_ref[:, e:2 * e])
    vbuf[:, CONV_PAD:CONV_PAD + tq, :] = (a * jax.nn.sigmoid(g)).reshape(bb, tq, e)
    z_sc[...] = _dot(hb, win_ref[:, 2 * e:3 * e])

    base = CONV_PAD - CONV_STATE
    for b in range(bb):
        for r0 in range(0, tq, CONV_ROWS):
            rc = min(CONV_ROWS, tq - r0)
            acc = jnp.broadcast_to(bdw_ref[...], (rc, e))
            for j in range(CONV_WIDTH):
                acc = acc + wdw_ref[j:j + 1, :] * vbuf[b, base + j + r0:base + j + r0 + rc, :]
            mu = jnp.mean(acc, axis=-1, keepdims=True)
            yc = acc - mu
            var = jnp.mean(yc * yc, axis=-1, keepdims=True)
            yn = yc * lax.rsqrt(var + EPS) * lng_ref[...] + lnb_ref[...]
            rows = slice(b * tq + r0, b * tq + r0 + rc)
            gated_sc[rows, :] = (jax.nn.silu(yn) * jax.nn.silu(z_sc[rows, :])).astype(BF16)

    out = _dot(gated_sc[...], wout_ref[...])
    y_ref[...] = (x + out).reshape(bb, tq, d)

    tail = vbuf[:, tq:tq + CONV_PAD, :]

    @pl.when(t == nt - 1)
    def _():
        cst_ref[...] = vbuf[:, tq + CONV_PAD - CONV_STATE:tq + CONV_PAD, :]

    vbuf[:, 0:CONV_PAD, :] = tail


def _conv_layer(x, state, norm_g, w_in, w_dw, b_dw, ln_g, ln_b, w_out, *, bb, tq):
    bsz, t, d = x.shape
    e = w_out.shape[0]
    grid = (bsz // bb, t // tq)
    return pl.pallas_call(
        _conv_layer_body,
        grid=grid,
        in_specs=[
            pl.BlockSpec((bb, tq, d), lambda i, j: (i, j, 0)),
            pl.BlockSpec((bb, CONV_STATE, e), lambda i, j: (i, 0, 0)),
            _const_spec((1, d)),
            _const_spec((d, 3 * e)),
            _const_spec((CONV_WIDTH, e)),
            _const_spec((1, e)),
            _const_spec((1, e)),
            _const_spec((1, e)),
            _const_spec((e, d)),
        ],
        out_specs=[
            pl.BlockSpec((bb, tq, d), lambda i, j: (i, j, 0)),
            pl.BlockSpec((bb, CONV_STATE, e), lambda i, j: (i, 0, 0)),
        ],
        out_shape=[
            jax.ShapeDtypeStruct((bsz, t, d), F32),
            jax.ShapeDtypeStruct((bsz, CONV_STATE, e), F32),
        ],
        scratch_shapes=[
            pltpu.VMEM((bb, CONV_PAD + tq, e), F32),
            pltpu.VMEM((bb * tq, e), F32),
            pltpu.VMEM((bb * tq, e), BF16),
        ],
        compiler_params=pltpu.CompilerParams(
            dimension_semantics=("arbitrary", "arbitrary"), vmem_limit_bytes=VMEM_LIMIT),
        name="conv_layer",
    )(x, state, norm_g, w_in, w_dw, b_dw, ln_g, ln_b, w_out)


def _fox_in_body(x_ref, g_ref, wq_ref, wk_ref, wv_ref, wz_ref, wfl_ref, wflt_ref, bfr_ref, bfc_ref,
                 qg_ref, kg_ref, sblk_ref, tri_ref,
                 q_ref, kf_ref, kb_ref, vf_ref, vb_ref, z_ref, lf_ref, ct_ref, carry,
                 *, tiles_per_seg):
    t = pl.program_id(0)
    tq = x_ref.shape[0]

    hb = (_rms(x_ref[...]) * g_ref[...]).astype(BF16)

    q = _dot(hb, wq_ref[...])
    qss = _dot((q * q).astype(BF16), sblk_ref[...])
    q_ref[...] = (q * lax.rsqrt(qss + EPS) * qg_ref[...]).astype(BF16)

    k = _dot(hb, wk_ref[...])
    kss = _dot((k * k).astype(BF16), sblk_ref[...])
    kn = k * lax.rsqrt(kss + EPS) * kg_ref[...]
    kf_ref[...] = kn
    kb_ref[...] = kn.astype(BF16)

    v = _dot(hb, wv_ref[...])
    vf_ref[...] = v
    vb_ref[...] = v.astype(BF16)

    z_ref[...] = _dot(hb, wz_ref[...])

    fl = _dot(hb, wfl_ref[...])
    lf_ref[...] = _log_sigmoid(fl + bfr_ref[...])[:, 0:N_HEADS]

    flt = _dot_nt(wflt_ref[...], hb)
    lft = _log_sigmoid(flt + bfc_ref[...])
    hi, mid, lo = _split3(lft)
    c3 = _dot(jnp.concatenate([hi, mid, lo], axis=0), tri_ref[...])
    first = (t % tiles_per_seg) == 0
    prev = jnp.where(first, 0.0, carry[:, 0:1])
    ct = c3[0:N_HEADS] + c3[N_HEADS:2 * N_HEADS] + c3[2 * N_HEADS:3 * N_HEADS] + prev
    ct_ref[...] = ct
    carry[...] = jnp.broadcast_to(ct[:, tq - 1:tq], carry.shape)


def _fox_in(x, norm_g, wq, wk, wv, wz, wfl, wflt, bfr, bfc, qg, kg, sblk, tri, *, tq, tiles_per_seg):
    n, d = x.shape
    e = wq.shape[1]
    row = lambda i: (i, 0)
    row_spec = lambda w: pl.BlockSpec((tq, w), row)
    return pl.pallas_call(
        functools.partial(_fox_in_body, tiles_per_seg=tiles_per_seg),
        grid=(n // tq,),
        in_specs=[
            row_spec(d),
            _const_spec((1, d)),
            _const_spec((d, e)), _const_spec((d, e)), _const_spec((d, e)), _const_spec((d, e)),
            _const_spec((d, 128)), _const_spec((N_HEADS, d)),
            _const_spec((1, 128)), _const_spec((N_HEADS, 1)),
            _const_spec((1, e)), _const_spec((1, e)),
            _const_spec((e, e)), _const_spec((tq, tq)),
        ],
        out_specs=[
            row_spec(e), row_spec(e), row_spec(e), row_spec(e), row_spec(e), row_spec(e),
            row_spec(N_HEADS),
            pl.BlockSpec((N_HEADS, tq), lambda i: (0, i)),
        ],
        out_shape=[
            jax.ShapeDtypeStruct((n, e), BF16),
            jax.ShapeDtypeStruct((n, e), F32),
            jax.ShapeDtypeStruct((n, e), BF16),
            jax.ShapeDtypeStruct((n, e), F32),
            jax.ShapeDtypeStruct((n, e), BF16),
            jax.ShapeDtypeStruct((n, e), F32),
            jax.ShapeDtypeStruct((n, N_HEADS), F32),
            jax.ShapeDtypeStruct((N_HEADS, n), F32),
        ],
        scratch_shapes=[pltpu.VMEM((N_HEADS, 128), F32)],
        compiler_params=pltpu.CompilerParams(
            dimension_semantics=("arbitrary",), vmem_limit_bytes=VMEM_LIMIT),
        name="fox_in",
    )(x, norm_g, wq, wk, wv, wz, wfl, wflt, bfr, bfc, qg, kg, sblk, tri)


def _attn_body(q_ref, k_ref, v_ref, ct_ref, o_ref, m_sc, l_sc, acc_sc, *, tk):
    qi = pl.program_id(2)
    tq = q_ref.shape[1]
    q2 = q_ref[0]
    lane = lax.broadcasted_iota(jnp.int32, (tq, 2 * HEAD_DIM), 1)
    row = lax.broadcasted_iota(jnp.int32, (tq, tk), 0)
    col = lax.broadcasted_iota(jnp.int32, (tq, tk), 1)
    kv_per_q = tq // tk

    for hh in range(2):
        own = (lane >= HEAD_DIM) if hh else (lane < HEAD_DIM)
        qh = jnp.where(own, q2, jnp.zeros_like(q2))
        m_sc[hh] = jnp.full((tq, 1), NEG_BIG, F32)
        l_sc[hh] = jnp.zeros((tq, 1), F32)
        acc_sc[hh] = jnp.zeros((tq, 2 * HEAD_DIM), F32)

        def step(j, masked, hh=hh, qh=qh):
            start = pl.multiple_of(j * tk, tk)
            kb = k_ref[0, pl.ds(start, tk), :]
            vb = v_ref[0, pl.ds(start, tk), :]
            s = _dot_nt(qh, kb)
            s = s - ct_ref[0, hh:hh + 1, pl.ds(start, tk)] * LOG2E
            if masked:
                off = j * tk - qi * tq
                s = jnp.where(col + off <= row, s, NEG_BIG)
            m_prev = m_sc[hh]
            m_new = jnp.maximum(m_prev, jnp.max(s, axis=-1, keepdims=True))
            alpha = jnp.exp2(m_prev - m_new)
            p = jnp.exp2(s - m_new)
            l_sc[hh] = alpha * l_sc[hh] + jnp.sum(p, axis=-1, keepdims=True)
            acc_sc[hh] = alpha * acc_sc[hh] + _dot(p.astype(BF16), vb)
            m_sc[hh] = m_new

        def body(j, carry):
            step(j, False)
            return carry

        lax.fori_loop(0, qi * kv_per_q, body, 0)
        for jj in range(kv_per_q):
            step(qi * kv_per_q + jj, True)

    o0 = acc_sc[0] / l_sc[0]
    o1 = acc_sc[1] / l_sc[1]
    o_ref[0] = jnp.where(lane < HEAD_DIM, o0, o1)


def _attention(q, k, v, ct, *, tq, tk):
    bsz, t, e = q.shape
    n_pairs = e // (2 * HEAD_DIM)
    w = 2 * HEAD_DIM
    return pl.pallas_call(
        functools.partial(_attn_body, tk=tk),
        grid=(bsz, n_pairs, t // tq),
        in_specs=[
            pl.BlockSpec((1, tq, w), lambda b, h, i: (b, i, h)),
            pl.BlockSpec((1, t, w), lambda b, h, i: (b, 0, h)),
            pl.BlockSpec((1, t, w), lambda b, h, i: (b, 0, h)),
            pl.BlockSpec((1, 2, t), lambda b, h, i: (h, 0, b)),
        ],
        out_specs=pl.BlockSpec((1, tq, w), lambda b, h, i: (b, i, h)),
        out_shape=jax.ShapeDtypeStruct((bsz, t, e), F32),
        scratch_shapes=[
            pltpu.VMEM((2, tq, 1), F32),
            pltpu.VMEM((2, tq, 1), F32),
            pltpu.VMEM((2, tq, w), F32),
        ],
        compiler_params=pltpu.CompilerParams(
            dimension_semantics=("arbitrary", "arbitrary", "arbitrary"), vmem_limit_bytes=VMEM_LIMIT),
        name="attn_prompt",
    )(q, k, v, ct)


def _attn_sample_body(q_ref, kn_ref, vn_ref, cn_ref, kc_ref, vc_ref, lft_ref, tri_ref, rep_ref, rept_ref,
                      o_ref, qbd_sc, m_sc, l_sc, acc_sc, carry):
    j = pl.program_id(1)
    nj = pl.num_programs(1)
    nq, e = q_ref.shape[1], q_ref.shape[2]
    nr = nq * N_HEADS
    tk = kc_ref.shape[1]
    rrow = lax.broadcasted_iota(jnp.int32, (nr, e), 0)
    rcol = lax.broadcasted_iota(jnp.int32, (nr, e), 1)
    own = (rrow % N_HEADS) == (rcol // HEAD_DIM)

    def update(s, vb):
        m_prev = m_sc[...]
        m_new = jnp.maximum(m_prev, jnp.max(s, axis=-1, keepdims=True))
        alpha = jnp.exp2(m_prev - m_new)
        p = jnp.exp2(s - m_new)
        l_sc[...] = alpha * l_sc[...] + jnp.sum(p, axis=-1, keepdims=True)
        acc_sc[...] = alpha * acc_sc[...] + _dot(p.astype(BF16), vb)
        m_sc[...] = m_new

    @pl.when(j == 0)
    def _():
        qrep = _dot(rep_ref[...], q_ref[0])
        qbd_sc[...] = jnp.where(own, qrep, 0.0).astype(BF16)
        m_sc[...] = jnp.full(m_sc.shape, NEG_BIG, F32)
        l_sc[...] = jnp.zeros(l_sc.shape, F32)
        acc_sc[...] = jnp.zeros(acc_sc.shape, F32)
        carry[...] = jnp.zeros(carry.shape, F32)
        s = _dot_nt(qbd_sc[...], kn_ref[0])
        s = s - jnp.tile(cn_ref[0], (nq, 1)) * LOG2E
        r2 = lax.broadcasted_iota(jnp.int32, (nr, nq), 0)
        c2 = lax.broadcasted_iota(jnp.int32, (nr, nq), 1)
        s = jnp.where(c2 <= r2 // N_HEADS, s, NEG_BIG)
        update(s, vn_ref[0])

    lft = lft_ref[0]
    hi, mid, lo = _split3(lft)
    c3 = _dot(jnp.concatenate([hi, mid, lo], axis=0), tri_ref[...])
    incl = c3[0:N_HEADS] + c3[N_HEADS:2 * N_HEADS] + c3[2 * N_HEADS:3 * N_HEADS] + carry[:, 0:1]
    bias = incl - lft
    carry[...] = jnp.broadcast_to(incl[:, 0:1], carry.shape)
    s = _dot_nt(qbd_sc[...], kc_ref[0].astype(BF16))
    s = s + jnp.tile(bias, (nq, 1)) * LOG2E
    update(s, vc_ref[0].astype(BF16))

    @pl.when(j == nj - 1)
    def _():
        of = jnp.where(own, acc_sc[...] / l_sc[...], 0.0)
        hi2 = of.astype(BF16)
        lo2 = (of - hi2.astype(F32)).astype(BF16)
        o_ref[0] = _dot(rept_ref[...], hi2) + _dot(rept_ref[...], lo2)


def _attention_sample(q, kn, vn, cn, kc, vc, lft, tri, rep, rept, *, tk):
    bsz, nq, e = q.shape
    p = kc.shape[1]
    nkb = p // tk
    nr = nq * N_HEADS
    new_spec = pl.BlockSpec((1, nq, e), lambda b, j: (b, 0, 0))
    return pl.pallas_call(
        _attn_sample_body,
        grid=(bsz, nkb),
        in_specs=[
            new_spec, new_spec, new_spec,
            pl.BlockSpec((1, N_HEADS, nq), lambda b, j: (b, 0, 0)),
            pl.BlockSpec((1, tk, e), lambda b, j: (b, nkb - 1 - j, 0)),
            pl.BlockSpec((1, tk, e), lambda b, j: (b, nkb - 1 - j, 0)),
            pl.BlockSpec((1, N_HEADS, tk), lambda b, j: (b, 0, nkb - 1 - j)),
            _const_spec((tk, tk)),
            _const_spec((nr, nq)),
            _const_spec((nq, nr)),
        ],
        out_specs=new_spec,
        out_shape=jax.ShapeDtypeStruct((bsz, nq, e), F32),
        scratch_shapes=[
            pltpu.VMEM((nr, e), BF16),
            pltpu.VMEM((nr, 1), F32),
            pltpu.VMEM((nr, 1), F32),
            pltpu.VMEM((nr, e), F32),
            pltpu.VMEM((N_HEADS, 128), F32),
        ],
        compiler_params=pltpu.CompilerParams(
            dimension_semantics=("arbitrary", "arbitrary"), vmem_limit_bytes=VMEM_LIMIT),
        name="attn_sample",
    )(q, kn, vn, cn, kc, vc, lft, tri, rep, rept)


def _fox_out_body(o_ref, z_ref, x_ref, w_ref, g_ref, y_ref):
    gated = (o_ref[...] * jax.nn.silu(z_ref[...])).astype(BF16)
    y = x_ref[...] + _dot(gated, w_ref[...])
    y_ref[...] = _rms(y) * g_ref[...]


def _fox_out(o, z, x, w_out, final_g, *, tq):
    n, e = o.shape
    d = w_out.shape[1]
    row = lambda i: (i, 0)
    return pl.pallas_call(
        _fox_out_body,
        grid=(n // tq,),
        in_specs=[
            pl.BlockSpec((tq, e), row), pl.BlockSpec((tq, e), row), pl.BlockSpec((tq, d), row),
            _const_spec((e, d)), _const_spec((1, d)),
        ],
        out_specs=pl.BlockSpec((tq, d), row),
        out_shape=jax.ShapeDtypeStruct((n, d), F32),
        compiler_params=pltpu.CompilerParams(
            dimension_semantics=("arbitrary",), vmem_limit_bytes=VMEM_LIMIT),
        name="fox_out",
    )(o, z, x, w_out, final_g)


def _upper_tri(n, seg):
    a = jnp.arange(n)[:, None]
    b = jnp.arange(n)[None, :]
    return ((a <= b) & (a // seg == b // seg)).astype(BF16)


def _tile_rows(n, pref):
    t = min(n, pref)
    assert n % t == 0
    return t


def kernel(x_prompt, x_sample, state_conv, cache_k, cache_v, cache_logf, norm_g, final_norm_g,
           w_conv_in, w_dw, b_dw, conv_ln_g, conv_ln_b, w_conv_out,
           w_fox_in, b_forget, q_norm_g, k_norm_g, w_fox_out):
    bp, tp, d = x_prompt.shape
    bs, ts, _ = x_sample.shape
    past = cache_k.shape[2]
    e = N_HEADS * HEAD_DIM
    assert state_conv.shape[0] == 1 and cache_k.shape[0] == 1 and norm_g.shape[0] == 2

    conv_w = (norm_g[0][None], w_conv_in[0].astype(BF16), w_dw[0], b_dw[0][None],
              conv_ln_g[0][None], conv_ln_b[0][None], w_conv_out[0].astype(BF16))
    zero_state = jnp.zeros((bp, CONV_STATE, e), F32)
    y1p, conv_p = _conv_layer(x_prompt, zero_state, *conv_w, bb=1, tq=_tile_rows(tp, 256))
    y1s, conv_s = _conv_layer(x_sample, state_conv[0], *conv_w, bb=_tile_rows(bs, 8), tq=ts)

    wf = w_fox_in[0]
    wq, wk, wv, wz = (wf[:, i * e:(i + 1) * e].astype(BF16) for i in range(4))
    wfl_cols = wf[:, 4 * e:]
    wfl = jnp.pad(wfl_cols, ((0, 0), (0, 128 - N_HEADS))).astype(BF16)
    wflt = wfl_cols.T.astype(BF16)
    bfr = jnp.pad(b_forget[0], (0, 128 - N_HEADS))[None]
    bfc = b_forget[0][:, None]
    qg = (jnp.tile(q_norm_g[0], N_HEADS) * (LOG2E / math.sqrt(HEAD_DIM)))[None]
    kg = jnp.tile(k_norm_g[0], N_HEADS)[None]
    hid = jnp.arange(e) // HEAD_DIM
    sblk = ((hid[:, None] == hid[None, :]).astype(F32) / HEAD_DIM).astype(BF16)
    fox_w = (norm_g[1][None], wq, wk, wv, wz, wfl, wflt, bfr, bfc, qg, kg, sblk)

    tqp = _tile_rows(tp, 256)
    qp, kfp, kbp, vfp, vbp, zp, lfp, ctp = _fox_in(
        y1p.reshape(bp * tp, d), *fox_w, _upper_tri(tqp, tqp), tq=tqp, tiles_per_seg=tp // tqp)
    ns = bs * ts
    qs, kfs, kbs, vfs, vbs, zs, lfs, cts = _fox_in(
        y1s.reshape(ns, d), *fox_w, _upper_tri(ns, ts), tq=ns, tiles_per_seg=1)

    ta = _tile_rows(tp, 256)
    op = _attention(qp.reshape(bp, tp, e), kbp.reshape(bp, tp, e), vbp.reshape(bp, tp, e),
                    ctp.reshape(N_HEADS // 2, 2, bp * tp), tq=ta, tk=ta)

    tks = _tile_rows(past, 512)
    a = jnp.arange(tks)
    tri_suffix = (a[:, None] >= a[None, :]).astype(BF16)
    r = jnp.arange(ts * N_HEADS)
    rep = (r[:, None] // N_HEADS == jnp.arange(ts)[None, :]).astype(BF16)
    cn = cts.reshape(N_HEADS, bs, ts).transpose(1, 0, 2)
    lft_past = jnp.swapaxes(cache_logf[0], 1, 2)
    os_ = _attention_sample(
        qs.reshape(bs, ts, e), kbs.reshape(bs, ts, e), vbs.reshape(bs, ts, e), cn,
        cache_k[0].reshape(bs, past, e), cache_v[0].reshape(bs, past, e), lft_past,
        tri_suffix, rep, rep.T, tk=tks)

    wo = w_fox_out[0].astype(BF16)
    fg = final_norm_g[None]
    yp = _fox_out(op.reshape(bp * tp, e), zp, y1p.reshape(bp * tp, d), wo, fg, tq=_tile_rows(bp * tp, 512))
    ys = _fox_out(os_.reshape(ns, e), zs, y1s.reshape(ns, d), wo, fg, tq=ns)

    return (yp.reshape(bp, tp, d), ys.reshape(bs, ts, d),
            conv_p[None], conv_s[None],
            kfp.reshape(1, bp, tp, N_HEADS, HEAD_DIM), vfp.reshape(1, bp, tp, N_HEADS, HEAD_DIM),
            lfp.reshape(1, bp, tp, N_HEADS),
            kfs.reshape(1, bs, ts, N_HEADS, HEAD_DIM), vfs.reshape(1, bs, ts, N_HEADS, HEAD_DIM),
            lfs.reshape(1, bs, ts, N_HEADS))
```

```python
import functools
import math

import jax
import jax.numpy as jnp
from jax import lax
from jax.experimental import pallas as pl
from jax.experimental.pallas import tpu as pltpu

N_HEADS = 16
HEAD_DIM = 64
CONV_WIDTH = 31
CONV_STATE = CONV_WIDTH - 1
EPS = 1e-6
LOG2E = 1.4426950408889634
NEG_BIG = -1e30

F32 = jnp.float32
BF16 = jnp.bfloat16

CONV_PAD = 32
CONV_ROWS = 32
VMEM_LIMIT = 56 * 1024 * 1024


def _dot(a, b):
    return jnp.dot(a, b, preferred_element_type=F32)


def _dot_nt(a, b):
    return lax.dot_general(a, b, (((1,), (1,)), ((), ())), preferred_element_type=F32)


def _split3(x):
    hi = x.astype(BF16)
    r1 = x - hi.astype(F32)
    mid = r1.astype(BF16)
    lo = (r1 - mid.astype(F32)).astype(BF16)
    return hi, mid, lo


def _log_sigmoid(x):
    return jnp.minimum(x, 0.0) - jnp.log1p(jnp.exp(-jnp.abs(x)))


def _rms(x):
    return x * lax.rsqrt(jnp.mean(x * x, axis=-1, keepdims=True) + EPS)


def _const_spec(shape):
    nd = len(shape)
    return pl.BlockSpec(shape, lambda *_: (0,) * nd)


def _conv_layer_body(x_ref, st_ref, g_ref, win_ref, wdw_ref, bdw_ref, lng_ref, lnb_ref, wout_ref,
                     y_ref, cst_ref, vbuf, shift_sc, z_sc, gated_sc):
    t = pl.program_id(1)
    nt = pl.num_programs(1)
    bb, tq, d = x_ref.shape
    e = wout_ref.shape[0]

    @pl.when(t == 0)
    def _():
        vbuf[:, 0:CONV_PAD, :] = jnp.zeros((bb, CONV_PAD, e), F32)
        vbuf[:, CONV_PAD - CONV_STATE:CONV_PAD, :] = st_ref[...]

    x = x_ref[...].reshape(bb * tq, d)
    hb = (_rms(x) * g_ref[...]).astype(BF16)
    a = _dot(hb, win_ref[:, 0:e])
    g = _dot(hb, win_ref[:, e:2 * e])
    vbuf[:, CONV_PAD:CONV_PAD + tq, :] = (a * jax.nn.sigmoid(g)).reshape(bb, tq, e)
    z_sc[...] = _dot(hb, win_ref[:, 2 * e:3 * e])

    first = CONV_PAD - CONV_STATE
    span = shift_sc.shape[1]
    for b in range(bb):
        for s in range(1, 8):
            shift_sc[s - 1] = vbuf[b, s:s + span, :]
        for r0 in range(0, tq, CONV_ROWS):
            rc = min(CONV_ROWS, tq - r0)
            acc = jnp.broadcast_to(bdw_ref[...], (rc, e))
            for j in range(CONV_WIDTH):
                s, a = (first + j) % 8, (first + j) // 8 * 8
                if s == 0:
                    tap = vbuf[b, a + r0:a + r0 + rc, :]
                else:
                    tap = shift_sc[s - 1, a + r0:a + r0 + rc, :]
                acc = acc + wdw_ref[j:j + 1, :] * tap
            mu = jnp.mean(acc, axis=-1, keepdims=True)
            yc = acc - mu
            var = jnp.mean(yc * yc, axis=-1, keepdims=True)
            yn = yc * lax.rsqrt(var + EPS) * lng_ref[...] + lnb_ref[...]
            rows = slice(b * tq + r0, b * tq + r0 + rc)
            gated_sc[rows, :] = (jax.nn.silu(yn) * jax.nn.silu(z_sc[rows, :])).astype(BF16)

    out = _dot(gated_sc[...], wout_ref[...])
    y_ref[...] = (x + out).reshape(bb, tq, d)

    tail = vbuf[:, tq:tq + CONV_PAD, :]

    @pl.when(t == nt - 1)
    def _():
        cst_ref[...] = vbuf[:, tq + CONV_PAD - CONV_STATE:tq + CONV_PAD, :]

    vbuf[:, 0:CONV_PAD, :] = tail


def _conv_layer(x, state, norm_g, w_in, w_dw, b_dw, ln_g, ln_b, w_out, *, bb, tq):
    bsz, t, d = x.shape
    e = w_out.shape[0]
    grid = (bsz // bb, t // tq)
    return pl.pallas_call(
        _conv_layer_body,
        grid=grid,
        in_specs=[
            pl.BlockSpec((bb, tq, d), lambda i, j: (i, j, 0)),
            pl.BlockSpec((bb, CONV_STATE, e), lambda i, j: (i, 0, 0)),
            _const_spec((1, d)),
            _const_spec((d, 3 * e)),
            _const_spec((CONV_WIDTH, e)),
            _const_spec((1, e)),
            _const_spec((1, e)),
            _const_spec((1, e)),
            _const_spec((e, d)),
        ],
        out_specs=[
            pl.BlockSpec((bb, tq, d), lambda i, j: (i, j, 0)),
            pl.BlockSpec((bb, CONV_STATE, e), lambda i, j: (i, 0, 0)),
        ],
        out_shape=[
            jax.ShapeDtypeStruct((bsz, t, d), F32),
            jax.ShapeDtypeStruct((bsz, CONV_STATE, e), F32),
        ],
        scratch_shapes=[
            pltpu.VMEM((bb, CONV_PAD + tq, e), F32),
            pltpu.VMEM((7, tq + CONV_PAD - 8, e), F32),
            pltpu.VMEM((bb * tq, e), F32),
            pltpu.VMEM((bb * tq, e), BF16),
        ],
        compiler_params=pltpu.CompilerParams(
            dimension_semantics=("arbitrary", "arbitrary"), vmem_limit_bytes=VMEM_LIMIT),
        name="conv_layer",
    )(x, state, norm_g, w_in, w_dw, b_dw, ln_g, ln_b, w_out)


def _fox_in_body(x_ref, g_ref, wq_ref, wk_ref, wv_ref, wz_ref, wfl_ref, wflt_ref, bfr_ref, bfc_ref,
                 qg_ref, kg_ref, sblk_ref, tri_ref,
                 q_ref, kf_ref, kb_ref, vf_ref, vb_ref, z_ref, lf_ref, ct_ref, carry,
                 *, tiles_per_seg):
    t = pl.program_id(0)
    tq = x_ref.shape[0]

    hb = (_rms(x_ref[...]) * g_ref[...]).astype(BF16)

    q = _dot(hb, wq_ref[...])
    qss = _dot((q * q).astype(BF16), sblk_ref[...])
    q_ref[...] = (q * lax.rsqrt(qss + EPS) * qg_ref[...]).astype(BF16)

    k = _dot(hb, wk_ref[...])
    kss = _dot((k * k).astype(BF16), sblk_ref[...])
    kn = k * lax.rsqrt(kss + EPS) * kg_ref[...]
    kf_ref[...] = kn
    kb_ref[...] = kn.astype(BF16)

    v = _dot(hb, wv_ref[...])
    vf_ref[...] = v
    vb_ref[...] = v.astype(BF16)

    z_ref[...] = _dot(hb, wz_ref[...])

    fl = _dot(hb, wfl_ref[...])
    lf_ref[...] = _log_sigmoid(fl + bfr_ref[...])[:, 0:N_HEADS]

    flt = _dot_nt(wflt_ref[...], hb)
    lft = _log_sigmoid(flt + bfc_ref[...])
    hi, mid, lo = _split3(lft)
    c3 = _dot(jnp.concatenate([hi, mid, lo], axis=0), tri_ref[...])
    first = (t % tiles_per_seg) == 0
    prev = jnp.where(first, 0.0, carry[:, 0:1])
    ct = c3[0:N_HEADS] + c3[N_HEADS:2 * N_HEADS] + c3[2 * N_HEADS:3 * N_HEADS] + prev
    ct_ref[...] = ct
    carry[...] = jnp.broadcast_to(ct[:, tq - 1:tq], carry.shape)


def _fox_in(x, norm_g, wq, wk, wv, wz, wfl, wflt, bfr, bfc, qg, kg, sblk, tri, *, tq, tiles_per_seg):
    n, d = x.shape
    e = wq.shape[1]
    row = lambda i: (i, 0)
    row_spec = lambda w: pl.BlockSpec((tq, w), row)
    return pl.pallas_call(
        functools.partial(_fox_in_body, tiles_per_seg=tiles_per_seg),
        grid=(n // tq,),
        in_specs=[
            row_spec(d),
            _const_spec((1, d)),
            _const_spec((d, e)), _const_spec((d, e)), _const_spec((d, e)), _const_spec((d, e)),
            _const_spec((d, 128)), _const_spec((N_HEADS, d)),
            _const_spec((1, 128)), _const_spec((N_HEADS, 1)),
            _const_spec((1, e)), _const_spec((1, e)),
            _const_spec((e, e)), _const_spec((tq, tq)),
        ],
        out_specs=[
            row_spec(e), row_spec(e), row_spec(e), row_spec(e), row_spec(e), row_spec(e),
            row_spec(N_HEADS),
            pl.BlockSpec((N_HEADS, tq), lambda i: (0, i)),
        ],
        out_shape=[
            jax.ShapeDtypeStruct((n, e), BF16),
            jax.ShapeDtypeStruct((n, e), F32),
            jax.ShapeDtypeStruct((n, e), BF16),
            jax.ShapeDtypeStruct((n, e), F32),
            jax.ShapeDtypeStruct((n, e), BF16),
            jax.ShapeDtypeStruct((n, e), F32),
            jax.ShapeDtypeStruct((n, N_HEADS), F32),
            jax.ShapeDtypeStruct((N_HEADS, n), F32),
        ],
        scratch_shapes=[pltpu.VMEM((N_HEADS, 128), F32)],
        compiler_params=pltpu.CompilerParams(
            dimension_semantics=("arbitrary",), vmem_limit_bytes=VMEM_LIMIT),
        name="fox_in",
    )(x, norm_g, wq, wk, wv, wz, wfl, wflt, bfr, bfc, qg, kg, sblk, tri)


AUG = 3


def _fox_in_prompt_body(x_ref, g_ref, wq_ref, wk_ref, wv_ref, wz_ref, wfl_ref, bfr_ref,
                        qg_ref, kg_ref, sblk_ref, ltri_ref, place_ref,
                        qa_ref, ka_ref, kf_ref, vf_ref, vt_ref, z_ref, lf_ref, carry):
    t = pl.program_id(1)
    tq = x_ref.shape[1]
    w = 2 * HEAD_DIM

    @pl.when(t == 0)
    def _():
        carry[...] = jnp.zeros(carry.shape, F32)

    hb = (_rms(x_ref[0]) * g_ref[...]).astype(BF16)

    q = _dot(hb, wq_ref[...])
    qss = _dot((q * q).astype(BF16), sblk_ref[...])
    qn = q * lax.rsqrt(qss + EPS) * qg_ref[...]

    k = _dot(hb, wk_ref[...])
    kss = _dot((k * k).astype(BF16), sblk_ref[...])
    kn = k * lax.rsqrt(kss + EPS) * kg_ref[...]
    kf_ref[0] = kn

    v = _dot(hb, wv_ref[...])
    vf_ref[0] = v
    vt_ref[0, 0] = v.T.astype(BF16)

    z_ref[0] = _dot(hb, wz_ref[...])

    fl = _dot(hb, wfl_ref[...])
    lf = _log_sigmoid(fl + bfr_ref[...])
    lf_ref[0] = lf[:, 0:N_HEADS]

    c3 = _dot(ltri_ref[...], jnp.concatenate(_split3(lf), axis=1))
    c = c3[:, 0:w] + c3[:, w:2 * w] + c3[:, 2 * w:3 * w] + carry[0:1, :]
    carry[0:1, :] = c[tq - 1:tq, :]
    aug = _dot(jnp.concatenate(_split3(c * (-LOG2E)), axis=1), place_ref[...])

    lane = lax.broadcasted_iota(jnp.int32, (tq, w), 1)
    q_tail = jnp.where(lane < HEAD_DIM + AUG, 1.0, 0.0)
    for h in range(N_HEADS):
        sl = slice((h // 2) * w, (h // 2 + 1) * w)
        qt, kt = qn[:, sl], kn[:, sl]
        if h % 2:
            qt = pltpu.roll(qt, HEAD_DIM, axis=1)
            kt = pltpu.roll(kt, HEAD_DIM, axis=1)
        qa_ref[0, h] = jnp.where(lane < HEAD_DIM, qt, q_tail).astype(BF16)
        ka_ref[0, h] = jnp.where(lane < HEAD_DIM, kt, aug[:, h * w:(h + 1) * w]).astype(BF16)


def _fox_in_prompt(x, norm_g, wq, wk, wv, wz, wfl, bfr, qg, kg, sblk, ltri, place, *, tq):
    bsz, t, d = x.shape
    e = wq.shape[1]
    w = 2 * HEAD_DIM
    tile = lambda b, i: (b, i, 0)
    slab = pl.BlockSpec((1, N_HEADS, tq, w), lambda b, i: (b, 0, i, 0))
    return pl.pallas_call(
        _fox_in_prompt_body,
        grid=(bsz, t // tq),
        in_specs=[
            pl.BlockSpec((1, tq, d), tile),
            _const_spec((1, d)),
            _const_spec((d, e)), _const_spec((d, e)), _const_spec((d, e)), _const_spec((d, e)),
            _const_spec((d, w)), _const_spec((1, w)),
            _const_spec((1, e)), _const_spec((1, e)),
            _const_spec((e, e)), _const_spec((tq, tq)), _const_spec((AUG * w, N_HEADS * w)),
        ],
        out_specs=[
            slab, slab,
            pl.BlockSpec((1, tq, e), tile), pl.BlockSpec((1, tq, e), tile),
            pl.BlockSpec((1, 1, e, tq), lambda b, i: (b, i, 0, 0)),
            pl.BlockSpec((1, tq, e), tile),
            pl.BlockSpec((1, tq, N_HEADS), tile),
        ],
        out_shape=[
            jax.ShapeDtypeStruct((bsz, N_HEADS, t, w), BF16),
            jax.ShapeDtypeStruct((bsz, N_HEADS, t, w), BF16),
            jax.ShapeDtypeStruct((bsz, t, e), F32),
            jax.ShapeDtypeStruct((bsz, t, e), F32),
            jax.ShapeDtypeStruct((bsz, t // tq, e, tq), BF16),
            jax.ShapeDtypeStruct((bsz, t, e), F32),
            jax.ShapeDtypeStruct((bsz, t, N_HEADS), F32),
        ],
        scratch_shapes=[pltpu.VMEM((8, w), F32)],
        compiler_params=pltpu.CompilerParams(
            dimension_semantics=("arbitrary", "arbitrary"), vmem_limit_bytes=VMEM_LIMIT),
        name="fox_in_prompt",
    )(x, norm_g, wq, wk, wv, wz, wfl, bfr, qg, kg, sblk, ltri, place)


ONES_ROWS = 16


def _attn_t_body(qa_ref, ka_ref, vt_ref, ones_ref, o_ref, st0, st1, *, tq):
    nh, t = qa_ref.shape[1], qa_ref.shape[2]
    tk = vt_ref.shape[3]
    assert tq == 2 * tk
    krow = lax.broadcasted_iota(jnp.int32, (tk, tq), 0)
    qcol = lax.broadcasted_iota(jnp.int32, (tk, tq), 1)

    def q_tile(a, carry):
        q0 = pl.multiple_of(a * tq, tq)
        qts = [qa_ref[0, h, pl.ds(q0, tq), :] for h in range(nh)]

        def scores(j, buf):
            k0 = pl.multiple_of(j * tk, tk)
            for h in range(nh):
                buf[h] = _dot_nt(ka_ref[0, h, pl.ds(k0, tk), :], qts[h])

        def update(j, buf, states, mask_off):
            out = []
            for h in range(nh):
                m, acc = states[h]
                st = buf[h]
                if mask_off is not None:
                    st = jnp.where(krow + mask_off <= qcol, st, NEG_BIG)
                m_new = jnp.maximum(m, jnp.max(st, axis=0, keepdims=True))
                alpha = jnp.exp2(m - m_new)
                p = jnp.exp2(st - m_new).astype(BF16)
                vt = jnp.concatenate(
                    [vt_ref[0, j, h * HEAD_DIM:(h + 1) * HEAD_DIM, :], ones_ref[...]], axis=0)
                out.append((m_new, alpha * acc + _dot(vt, p)))
            return tuple(out)

        def two_steps(i, states):
            scores(2 * i + 1, st1)
            states = update(2 * i, st0, states, None)
            scores(2 * i + 2, st0)
            return update(2 * i + 1, st1, states, None)

        init = tuple((jnp.full((1, tq), NEG_BIG, F32), jnp.zeros((HEAD_DIM + ONES_ROWS, tq), F32))
                     for _ in range(nh))
        scores(0, st0)
        states = lax.fori_loop(0, a, two_steps, init)
        scores(2 * a + 1, st1)
        states = update(2 * a, st0, states, 0)
        states = update(2 * a + 1, st1, states, tk)
        ot = jnp.concatenate(
            [acc[0:HEAD_DIM] / acc[HEAD_DIM:HEAD_DIM + 1] for (_, acc) in states], axis=0)
        o_ref[0, pl.ds(q0, tq), :] = ot.T
        return carry

    lax.fori_loop(0, t // tq, q_tile, 0)


def _attention_t(qa, ka, vt, *, heads_per_step):
    bsz, nh, t, w = qa.shape
    nt, e, tk = vt.shape[1], vt.shape[2], vt.shape[3]
    g = heads_per_step
    tq = 2 * tk
    ones = (jnp.arange(ONES_ROWS)[:, None] == 0).astype(BF16) * jnp.ones((1, tk), BF16)
    return pl.pallas_call(
        functools.partial(_attn_t_body, tq=tq),
        grid=(bsz, nh // g),
        in_specs=[
            pl.BlockSpec((1, g, t, w), lambda b, h: (b, h, 0, 0)),
            pl.BlockSpec((1, g, t, w), lambda b, h: (b, h, 0, 0)),
            pl.BlockSpec((1, nt, g * HEAD_DIM, tk), lambda b, h: (b, 0, h, 0)),
            _const_spec((ONES_ROWS, tk)),
        ],
        out_specs=pl.BlockSpec((1, t, g * HEAD_DIM), lambda b, h: (b, 0, h)),
        out_shape=jax.ShapeDtypeStruct((bsz, t, e), F32),
        scratch_shapes=[pltpu.VMEM((g, tk, tq), F32), pltpu.VMEM((g, tk, tq), F32)],
        compiler_params=pltpu.CompilerParams(
            dimension_semantics=("arbitrary", "arbitrary"), vmem_limit_bytes=VMEM_LIMIT),
        name="attn_prompt",
    )(qa, ka, vt, ones)


def _attn_sample_body(q_ref, kn_ref, vn_ref, cn_ref, kc_ref, vc_ref, lft_ref, tri_ref, rep_ref, rept_ref,
                      o_ref, qbd_sc, m_sc, l_sc, acc_sc, carry):
    j = pl.program_id(1)
    nj = pl.num_programs(1)
    nq, e = q_ref.shape[1], q_ref.shape[2]
    nr = nq * N_HEADS
    tk = kc_ref.shape[1]
    rrow = lax.broadcasted_iota(jnp.int32, (nr, e), 0)
    rcol = lax.broadcasted_iota(jnp.int32, (nr, e), 1)
    own = (rrow % N_HEADS) == (rcol // HEAD_DIM)

    def update(s, vb):
        m_prev = m_sc[...]
        m_new = jnp.maximum(m_prev, jnp.max(s, axis=-1, keepdims=True))
        alpha = jnp.exp2(m_prev - m_new)
        p = jnp.exp2(s - m_new)
        l_sc[...] = alpha * l_sc[...] + jnp.sum(p, axis=-1, keepdims=True)
        acc_sc[...] = alpha * acc_sc[...] + _dot(p.astype(BF16), vb)
        m_sc[...] = m_new

    @pl.when(j == 0)
    def _():
        qrep = _dot(rep_ref[...], q_ref[0])
        qbd_sc[...] = jnp.where(own, qrep, 0.0).astype(BF16)
        m_sc[...] = jnp.full(m_sc.shape, NEG_BIG, F32)
        l_sc[...] = jnp.zeros(l_sc.shape, F32)
        acc_sc[...] = jnp.zeros(acc_sc.shape, F32)
        carry[...] = jnp.zeros(carry.shape, F32)
        s = _dot_nt(qbd_sc[...], kn_ref[0])
        s = s - jnp.tile(cn_ref[0], (nq, 1)) * LOG2E
        r2 = lax.broadcasted_iota(jnp.int32, (nr, nq), 0)
        c2 = lax.broadcasted_iota(jnp.int32, (nr, nq), 1)
        s = jnp.where(c2 <= r2 // N_HEADS, s, NEG_BIG)
        update(s, vn_ref[0])

    lft = lft_ref[0]
    hi, mid, lo = _split3(lft)
    c3 = _dot(jnp.concatenate([hi, mid, lo], axis=0), tri_ref[...])
    incl = c3[0:N_HEADS] + c3[N_HEADS:2 * N_HEADS] + c3[2 * N_HEADS:3 * N_HEADS] + carry[:, 0:1]
    bias = incl - lft
    carry[...] = jnp.broadcast_to(incl[:, 0:1], carry.shape)
    s = _dot_nt(qbd_sc[...], kc_ref[0].astype(BF16))
    s = s + jnp.tile(bias, (nq, 1)) * LOG2E
    update(s, vc_ref[0].astype(BF16))

    @pl.when(j == nj - 1)
    def _():
        of = jnp.where(own, acc_sc[...] / l_sc[...], 0.0)
        hi2 = of.astype(BF16)
        lo2 = (of - hi2.astype(F32)).astype(BF16)
        o_ref[0] = _dot(rept_ref[...], hi2) + _dot(rept_ref[...], lo2)


def _attention_sample(q, kn, vn, cn, kc, vc, lft, tri, rep, rept, *, tk):
    bsz, nq, e = q.shape
    p = kc.shape[1]
    nkb = p // tk
    nr = nq * N_HEADS
    new_spec = pl.BlockSpec((1, nq, e), lambda b, j: (b, 0, 0))
    return pl.pallas_call(
        _attn_sample_body,
        grid=(bsz, nkb),
        in_specs=[
            new_spec, new_spec, new_spec,
            pl.BlockSpec((1, N_HEADS, nq), lambda b, j: (b, 0, 0)),
            pl.BlockSpec((1, tk, e), lambda b, j: (b, nkb - 1 - j, 0)),
            pl.BlockSpec((1, tk, e), lambda b, j: (b, nkb - 1 - j, 0)),
            pl.BlockSpec((1, N_HEADS, tk), lambda b, j: (b, 0, nkb - 1 - j)),
            _const_spec((tk, tk)),
            _const_spec((nr, nq)),
            _const_spec((nq, nr)),
        ],
        out_specs=new_spec,
        out_shape=jax.ShapeDtypeStruct((bsz, nq, e), F32),
        scratch_shapes=[
            pltpu.VMEM((nr, e), BF16),
            pltpu.VMEM((nr, 1), F32),
            pltpu.VMEM((nr, 1), F32),
            pltpu.VMEM((nr, e), F32),
            pltpu.VMEM((N_HEADS, 128), F32),
        ],
        compiler_params=pltpu.CompilerParams(
            dimension_semantics=("arbitrary", "arbitrary"), vmem_limit_bytes=VMEM_LIMIT),
        name="attn_sample",
    )(q, kn, vn, cn, kc, vc, lft, tri, rep, rept)


def _fox_out_body(o_ref, z_ref, x_ref, w_ref, g_ref, y_ref):
    gated = (o_ref[...] * jax.nn.silu(z_ref[...])).astype(BF16)
    y = x_ref[...] + _dot(gated, w_ref[...])
    y_ref[...] = _rms(y) * g_ref[...]


def _fox_out(o, z, x, w_out, final_g, *, tq):
    n, e = o.shape
    d = w_out.shape[1]
    row = lambda i: (i, 0)
    return pl.pallas_call(
        _fox_out_body,
        grid=(n // tq,),
        in_specs=[
            pl.BlockSpec((tq, e), row), pl.BlockSpec((tq, e), row), pl.BlockSpec((tq, d), row),
            _const_spec((e, d)), _const_spec((1, d)),
        ],
        out_specs=pl.BlockSpec((tq, d), row),
        out_shape=jax.ShapeDtypeStruct((n, d), F32),
        compiler_params=pltpu.CompilerParams(
            dimension_semantics=("arbitrary",), vmem_limit_bytes=VMEM_LIMIT),
        name="fox_out",
    )(o, z, x, w_out, final_g)


def _upper_tri(n, seg):
    a = jnp.arange(n)[:, None]
    b = jnp.arange(n)[None, :]
    return ((a <= b) & (a // seg == b // seg)).astype(BF16)


def _tile_rows(n, pref):
    t = min(n, pref)
    assert n % t == 0
    return t


def kernel(x_prompt, x_sample, state_conv, cache_k, cache_v, cache_logf, norm_g, final_norm_g,
           w_conv_in, w_dw, b_dw, conv_ln_g, conv_ln_b, w_conv_out,
           w_fox_in, b_forget, q_norm_g, k_norm_g, w_fox_out):
    bp, tp, d = x_prompt.shape
    bs, ts, _ = x_sample.shape
    past = cache_k.shape[2]
    e = N_HEADS * HEAD_DIM
    assert state_conv.shape[0] == 1 and cache_k.shape[0] == 1 and norm_g.shape[0] == 2

    conv_w = (norm_g[0][None], w_conv_in[0].astype(BF16), w_dw[0], b_dw[0][None],
              conv_ln_g[0][None], conv_ln_b[0][None], w_conv_out[0].astype(BF16))
    zero_state = jnp.zeros((bp, CONV_STATE, e), F32)
    y1p, conv_p = _conv_layer(x_prompt, zero_state, *conv_w, bb=1, tq=_tile_rows(tp, 256))
    y1s, conv_s = _conv_layer(x_sample, state_conv[0], *conv_w, bb=_tile_rows(bs, 8), tq=ts)

    wf = w_fox_in[0]
    wq, wk, wv, wz = (wf[:, i * e:(i + 1) * e].astype(BF16) for i in range(4))
    wfl_cols = wf[:, 4 * e:]
    wfl = jnp.pad(wfl_cols, ((0, 0), (0, 128 - N_HEADS))).astype(BF16)
    wflt = wfl_cols.T.astype(BF16)
    bfr = jnp.pad(b_forget[0], (0, 128 - N_HEADS))[None]
    bfc = b_forget[0][:, None]
    qg = (jnp.tile(q_norm_g[0], N_HEADS) * (LOG2E / math.sqrt(HEAD_DIM)))[None]
    kg = jnp.tile(k_norm_g[0], N_HEADS)[None]
    hid = jnp.arange(e) // HEAD_DIM
    sblk = ((hid[:, None] == hid[None, :]).astype(F32) / HEAD_DIM).astype(BF16)
    fox_w = (norm_g[1][None], wq, wk, wv, wz, wfl, wflt, bfr, bfc, qg, kg, sblk)

    tqp = _tile_rows(tp, 256)
    w = 2 * HEAD_DIM
    rr = jnp.arange(tqp)
    ltri = (rr[None, :] <= rr[:, None]).astype(BF16)
    src, dst = jnp.arange(AUG * w), jnp.arange(N_HEADS * w)
    term, head = src // w, src % w
    place = ((head[:, None] < N_HEADS)
             & (dst[None, :] == head[:, None] * w + HEAD_DIM + term[:, None])).astype(BF16)
    qap, kap, kfp, vfp, vtp, zp, lfp = _fox_in_prompt(
        y1p, norm_g[1][None], wq, wk, wv, wz, wfl, bfr, qg, kg, sblk, ltri, place, tq=tqp)
    ns = bs * ts
    qs, kfs, kbs, vfs, vbs, zs, lfs, cts = _fox_in(
        y1s.reshape(ns, d), *fox_w, _upper_tri(ns, ts), tq=ns, tiles_per_seg=1)

    op = _attention_t(qap, kap, vtp, heads_per_step=4)

    tks = _tile_rows(past, 512)
    a = jnp.arange(tks)
    tri_suffix = (a[:, None] >= a[None, :]).astype(BF16)
    r = jnp.arange(ts * N_HEADS)
    rep = (r[:, None] // N_HEADS == jnp.arange(ts)[None, :]).astype(BF16)
    cn = cts.reshape(N_HEADS, bs, ts).transpose(1, 0, 2)
    lft_past = jnp.swapaxes(cache_logf[0], 1, 2)
    os_ = _attention_sample(
        qs.reshape(bs, ts, e), kbs.reshape(bs, ts, e), vbs.reshape(bs, ts, e), cn,
        cache_k[0].reshape(bs, past, e), cache_v[0].reshape(bs, past, e), lft_past,
        tri_suffix, rep, rep.T, tk=tks)

    wo = w_fox_out[0].astype(BF16)
    fg = final_norm_g[None]
    yp = _fox_out(op.reshape(bp * tp, e), zp.reshape(bp * tp, e), y1p.reshape(bp * tp, d), wo, fg,
                  tq=_tile_rows(bp * tp, 512))
    ys = _fox_out(os_.reshape(ns, e), zs, y1s.reshape(ns, d), wo, fg, tq=ns)

    return (yp.reshape(bp, tp, d), ys.reshape(bs, ts, d),
            conv_p[None], conv_s[None],
            kfp.reshape(1, bp, tp, N_HEADS, HEAD_DIM), vfp.reshape(1, bp, tp, N_HEADS, HEAD_DIM),
            lfp.reshape(1, bp, tp, N_HEADS),
            kfs.reshape(1, bs, ts, N_HEADS, HEAD_DIM), vfs.reshape(1, bs, ts, N_HEADS, HEAD_DIM),
            lfs.reshape(1, bs, ts, N_HEADS))
```

```python
import functools
import math

import jax
import jax.numpy as jnp
from jax import lax
from jax.experimental import pallas as pl
from jax.experimental.pallas import tpu as pltpu

N_HEADS = 16
HEAD_DIM = 64
CONV_WIDTH = 31
CONV_STATE = CONV_WIDTH - 1
EPS = 1e-6
SUBLANES = 8
LOG2E = 1.4426950408889634
NEG_BIG = -1e30

F32 = jnp.float32
BF16 = jnp.bfloat16

CONV_PAD = 32
CONV_ROWS = 64
VMEM_LIMIT = 56 * 1024 * 1024


def _dot(a, b):
    return jnp.dot(a, b, preferred_element_type=F32)


def _dot_nt(a, b):
    return lax.dot_general(a, b, (((1,), (1,)), ((), ())), preferred_element_type=F32)


def _split3(x):
    hi = x.astype(BF16)
    r1 = x - hi.astype(F32)
    mid = r1.astype(BF16)
    lo = (r1 - mid.astype(F32)).astype(BF16)
    return hi, mid, lo


def _log_sigmoid(x):
    return jnp.minimum(x, 0.0) - jnp.log1p(jnp.exp(-jnp.abs(x)))


def _rms(x):
    return x * lax.rsqrt(jnp.mean(x * x, axis=-1, keepdims=True) + EPS)


def _head_rms(x, hsum_ref, hexp_ref):
    ms = _dot((x * x).astype(BF16), hsum_ref[...])
    r = lax.rsqrt(ms + EPS)
    hi = r.astype(BF16)
    lo = (r - hi.astype(F32)).astype(BF16)
    return x * _dot(jnp.concatenate([hi, lo], axis=1), hexp_ref[...])


def _rows(ref, n):
    return jnp.tile(ref[...], (n // SUBLANES, 1))


def _spread(v):
    return jnp.broadcast_to(v[None, :], (SUBLANES, v.shape[0]))


def _const_spec(shape):
    nd = len(shape)
    return pl.BlockSpec(shape, lambda *_: (0,) * nd)


def _conv_layer_body(x_ref, st_ref, g_ref, win_ref, wdw_ref, bdw_ref, lng_ref, lnb_ref, wout_ref,
                     y_ref, cst_ref, vbuf, shift_sc, z_sc, gated_sc):
    t = pl.program_id(1)
    nt = pl.num_programs(1)
    bb, tq, d = x_ref.shape
    e = wout_ref.shape[0]

    @pl.when(t == 0)
    def _():
        vbuf[:, 0:CONV_PAD, :] = jnp.zeros((bb, CONV_PAD, e), F32)
        vbuf[:, CONV_PAD - CONV_STATE:CONV_PAD, :] = st_ref[...]

    x = x_ref[...].reshape(bb * tq, d)
    hb = (_rms(x) * _rows(g_ref, bb * tq)).astype(BF16)
    a = _dot(hb, win_ref[:, 0:e])
    g = _dot(hb, win_ref[:, e:2 * e])
    vbuf[:, CONV_PAD:CONV_PAD + tq, :] = (a * jax.nn.sigmoid(g)).reshape(bb, tq, e)
    z_sc[...] = _dot(hb, win_ref[:, 2 * e:3 * e])

    first = CONV_PAD - CONV_STATE
    span = shift_sc.shape[1]
    for b in range(bb):
        for s in range(1, 8):
            shift_sc[s - 1] = vbuf[b, s:s + span, :]
        for r0 in range(0, tq, CONV_ROWS):
            rc = min(CONV_ROWS, tq - r0)
            acc = _rows(bdw_ref, rc)
            for j in range(CONV_WIDTH):
                s, a = (first + j) % 8, (first + j) // 8 * 8
                if s == 0:
                    tap = vbuf[b, a + r0:a + r0 + rc, :]
                else:
                    tap = shift_sc[s - 1, a + r0:a + r0 + rc, :]
                acc = acc + jnp.tile(wdw_ref[j], (rc // 8, 1)) * tap
            mu = jnp.mean(acc, axis=-1, keepdims=True)
            yc = acc - mu
            var = jnp.mean(yc * yc, axis=-1, keepdims=True)
            yn = yc * lax.rsqrt(var + EPS) * _rows(lng_ref, rc) + _rows(lnb_ref, rc)
            rows = slice(b * tq + r0, b * tq + r0 + rc)
            gated_sc[rows, :] = (jax.nn.silu(yn) * jax.nn.silu(z_sc[rows, :])).astype(BF16)

    out = _dot(gated_sc[...], wout_ref[...])
    y_ref[...] = (x + out).reshape(bb, tq, d)

    tail = vbuf[:, tq:tq + CONV_PAD, :]

    @pl.when(t == nt - 1)
    def _():
        cst_ref[...] = vbuf[:, tq + CONV_PAD - CONV_STATE:tq + CONV_PAD, :]

    vbuf[:, 0:CONV_PAD, :] = tail


def _conv_layer(x, state, norm_g, w_in, w_dw, b_dw, ln_g, ln_b, w_out, *, bb, tq):
    bsz, t, d = x.shape
    e = w_out.shape[0]
    grid = (bsz // bb, t // tq)
    return pl.pallas_call(
        _conv_layer_body,
        grid=grid,
        in_specs=[
            pl.BlockSpec((bb, tq, d), lambda i, j: (i, j, 0)),
            pl.BlockSpec((bb, CONV_STATE, e), lambda i, j: (i, 0, 0)),
            _const_spec((SUBLANES, d)),
            _const_spec((d, 3 * e)),
            _const_spec((CONV_WIDTH, 8, e)),
            _const_spec((SUBLANES, e)),
            _const_spec((SUBLANES, e)),
            _const_spec((SUBLANES, e)),
            _const_spec((e, d)),
        ],
        out_specs=[
            pl.BlockSpec((bb, tq, d), lambda i, j: (i, j, 0)),
            pl.BlockSpec((bb, CONV_STATE, e), lambda i, j: (i, 0, 0)),
        ],
        out_shape=[
            jax.ShapeDtypeStruct((bsz, t, d), F32),
            jax.ShapeDtypeStruct((bsz, CONV_STATE, e), F32),
        ],
        scratch_shapes=[
            pltpu.VMEM((bb, CONV_PAD + tq, e), F32),
            pltpu.VMEM((7, tq + CONV_PAD - 8, e), F32),
            pltpu.VMEM((bb * tq, e), F32),
            pltpu.VMEM((bb * tq, e), BF16),
        ],
        compiler_params=pltpu.CompilerParams(
            dimension_semantics=("arbitrary", "arbitrary"), vmem_limit_bytes=VMEM_LIMIT),
        name="conv_layer",
    )(x, state, norm_g, w_in, w_dw, b_dw, ln_g, ln_b, w_out)


def _fox_in_body(x_ref, g_ref, wq_ref, wk_ref, wv_ref, wz_ref, wfl_ref, wflt_ref, bfr_ref, bfc_ref,
                 qg_ref, kg_ref, hsum_ref, hexp_ref, tri_ref,
                 q_ref, kf_ref, kb_ref, vf_ref, vb_ref, z_ref, lf_ref, ct_ref, carry,
                 *, tiles_per_seg):
    t = pl.program_id(0)
    tq = x_ref.shape[0]

    hb = (_rms(x_ref[...]) * _rows(g_ref, tq)).astype(BF16)

    q_ref[...] = (_head_rms(_dot(hb, wq_ref[...]), hsum_ref, hexp_ref) * _rows(qg_ref, tq)).astype(BF16)
    kn = _head_rms(_dot(hb, wk_ref[...]), hsum_ref, hexp_ref) * _rows(kg_ref, tq)
    kf_ref[...] = kn
    kb_ref[...] = kn.astype(BF16)

    v = _dot(hb, wv_ref[...])
    vf_ref[...] = v
    vb_ref[...] = v.astype(BF16)

    z_ref[...] = _dot(hb, wz_ref[...])

    fl = _dot(hb, wfl_ref[...])
    lf_ref[...] = _log_sigmoid(fl + _rows(bfr_ref, tq))[:, 0:N_HEADS]

    flt = _dot_nt(wflt_ref[...], hb)
    lft = _log_sigmoid(flt + bfc_ref[...])
    hi, mid, lo = _split3(lft)
    c3 = _dot(jnp.concatenate([hi, mid, lo], axis=0), tri_ref[...])
    first = (t % tiles_per_seg) == 0
    prev = jnp.where(first, 0.0, carry[:, 0:1])
    ct = c3[0:N_HEADS] + c3[N_HEADS:2 * N_HEADS] + c3[2 * N_HEADS:3 * N_HEADS] + prev
    ct_ref[...] = ct
    carry[...] = jnp.broadcast_to(ct[:, tq - 1:tq], carry.shape)


def _fox_in(x, norm_g, wq, wk, wv, wz, wfl, wflt, bfr, bfc, qg, kg, hsum, hexp, tri, *, tq, tiles_per_seg):
    n, d = x.shape
    e = wq.shape[1]
    row = lambda i: (i, 0)
    row_spec = lambda w: pl.BlockSpec((tq, w), row)
    return pl.pallas_call(
        functools.partial(_fox_in_body, tiles_per_seg=tiles_per_seg),
        grid=(n // tq,),
        in_specs=[
            row_spec(d),
            _const_spec((SUBLANES, d)),
            _const_spec((d, e)), _const_spec((d, e)), _const_spec((d, e)), _const_spec((d, e)),
            _const_spec((d, 128)), _const_spec((N_HEADS, d)),
            _const_spec((SUBLANES, 128)), _const_spec((N_HEADS, 1)),
            _const_spec((SUBLANES, e)), _const_spec((SUBLANES, e)),
            _const_spec((e, 128)), _const_spec((256, e)), _const_spec((tq, tq)),
        ],
        out_specs=[
            row_spec(e), row_spec(e), row_spec(e), row_spec(e), row_spec(e), row_spec(e),
            row_spec(N_HEADS),
            pl.BlockSpec((N_HEADS, tq), lambda i: (0, i)),
        ],
        out_shape=[
            jax.ShapeDtypeStruct((n, e), BF16),
            jax.ShapeDtypeStruct((n, e), F32),
            jax.ShapeDtypeStruct((n, e), BF16),
            jax.ShapeDtypeStruct((n, e), F32),
            jax.ShapeDtypeStruct((n, e), BF16),
            jax.ShapeDtypeStruct((n, e), F32),
            jax.ShapeDtypeStruct((n, N_HEADS), F32),
            jax.ShapeDtypeStruct((N_HEADS, n), F32),
        ],
        scratch_shapes=[pltpu.VMEM((N_HEADS, 128), F32)],
        compiler_params=pltpu.CompilerParams(
            dimension_semantics=("arbitrary",), vmem_limit_bytes=VMEM_LIMIT),
        name="fox_in",
    )(x, norm_g, wq, wk, wv, wz, wfl, wflt, bfr, bfc, qg, kg, hsum, hexp, tri)


AUG = 3


def _fox_in_prompt_body(x_ref, g_ref, wq_ref, wk_ref, wv_ref, wz_ref, wfl_ref, bfr_ref,
                        qg_ref, kg_ref, hsum_ref, hexp_ref, ltri_ref, place_ref,
                        qa_ref, ka_ref, kf_ref, vf_ref, vt_ref, z_ref, lf_ref, carry):
    t = pl.program_id(1)
    tq = x_ref.shape[1]
    w = 2 * HEAD_DIM

    @pl.when(t == 0)
    def _():
        carry[...] = jnp.zeros(carry.shape, F32)

    hb = (_rms(x_ref[0]) * _rows(g_ref, tq)).astype(BF16)

    qn = _head_rms(_dot(hb, wq_ref[...]), hsum_ref, hexp_ref) * _rows(qg_ref, tq)
    kn = _head_rms(_dot(hb, wk_ref[...]), hsum_ref, hexp_ref) * _rows(kg_ref, tq)
    kf_ref[0] = kn

    v = _dot(hb, wv_ref[...])
    vf_ref[0] = v
    vt_ref[0, 0] = v.T.astype(BF16)

    z_ref[0] = _dot(hb, wz_ref[...])

    lane = lax.broadcasted_iota(jnp.int32, (tq, w), 1)
    live = lane < N_HEADS
    fl = _dot(hb, wfl_ref[...])
    lf = jnp.where(live, _log_sigmoid(fl + _rows(bfr_ref, tq)), 0.0)
    lf_ref[0] = lf[:, 0:N_HEADS]

    def pack3(x):
        hi, mid, lo = (term.astype(F32) for term in _split3(x))
        return (hi + pltpu.roll(mid, N_HEADS, axis=1) + pltpu.roll(lo, 2 * N_HEADS, axis=1)).astype(BF16)

    c3 = _dot(ltri_ref[...], pack3(lf))
    c = c3 + pltpu.roll(c3, w - N_HEADS, axis=1) + pltpu.roll(c3, w - 2 * N_HEADS, axis=1)
    c = jnp.where(live, c + carry[0:1, :], 0.0)
    carry[0:1, :] = c[tq - 1:tq, :]
    aug = _dot(pack3(c * (-LOG2E)), place_ref[...])

    q_tail = jnp.where(lane < HEAD_DIM + AUG, 1.0, 0.0)
    for h in range(N_HEADS):
        sl = slice((h // 2) * w, (h // 2 + 1) * w)
        qt, kt = qn[:, sl], kn[:, sl]
        if h % 2:
            qt = pltpu.roll(qt, HEAD_DIM, axis=1)
            kt = pltpu.roll(kt, HEAD_DIM, axis=1)
        qa_ref[0, h] = jnp.where(lane < HEAD_DIM, qt, q_tail).astype(BF16)
        ka_ref[0, h] = jnp.where(lane < HEAD_DIM, kt, aug[:, h * w:(h + 1) * w]).astype(BF16)


def _fox_in_prompt(x, norm_g, wq, wk, wv, wz, wfl, bfr, qg, kg, hsum, hexp, ltri, place, *, tq):
    bsz, t, d = x.shape
    e = wq.shape[1]
    w = 2 * HEAD_DIM
    tile = lambda b, i: (b, i, 0)
    slab = pl.BlockSpec((1, N_HEADS, tq, w), lambda b, i: (b, 0, i, 0))
    return pl.pallas_call(
        _fox_in_prompt_body,
        grid=(bsz, t // tq),
        in_specs=[
            pl.BlockSpec((1, tq, d), tile),
            _const_spec((SUBLANES, d)),
            _const_spec((d, e)), _const_spec((d, e)), _const_spec((d, e)), _const_spec((d, e)),
            _const_spec((d, w)), _const_spec((SUBLANES, w)),
            _const_spec((SUBLANES, e)), _const_spec((SUBLANES, e)),
            _const_spec((e, w)), _const_spec((2 * w, e)), _const_spec((tq, tq)),
            _const_spec((w, N_HEADS * w)),
        ],
        out_specs=[
            slab, slab,
            pl.BlockSpec((1, tq, e), tile), pl.BlockSpec((1, tq, e), tile),
            pl.BlockSpec((1, 1, e, tq), lambda b, i: (b, i, 0, 0)),
            pl.BlockSpec((1, tq, e), tile),
            pl.BlockSpec((1, tq, N_HEADS), tile),
        ],
        out_shape=[
            jax.ShapeDtypeStruct((bsz, N_HEADS, t, w), BF16),
            jax.ShapeDtypeStruct((bsz, N_HEADS, t, w), BF16),
            jax.ShapeDtypeStruct((bsz, t, e), F32),
            jax.ShapeDtypeStruct((bsz, t, e), F32),
            jax.ShapeDtypeStruct((bsz, t // tq, e, tq), BF16),
            jax.ShapeDtypeStruct((bsz, t, e), F32),
            jax.ShapeDtypeStruct((bsz, t, N_HEADS), F32),
        ],
        scratch_shapes=[pltpu.VMEM((8, w), F32)],
        compiler_params=pltpu.CompilerParams(
            dimension_semantics=("arbitrary", "arbitrary"), vmem_limit_bytes=VMEM_LIMIT),
        name="fox_in_prompt",
    )(x, norm_g, wq, wk, wv, wz, wfl, bfr, qg, kg, hsum, hexp, ltri, place)


ONES_ROWS = 16


def _attn_t_body(qa_ref, ka_ref, vt_ref, ones_ref, o_ref, st0, st1, *, tq):
    nh, t = qa_ref.shape[1], qa_ref.shape[2]
    tk = vt_ref.shape[3]
    assert tq == 2 * tk
    krow = lax.broadcasted_iota(jnp.int32, (tk, tq), 0)
    qcol = lax.broadcasted_iota(jnp.int32, (tk, tq), 1)

    def q_tile(a, carry):
        q0 = pl.multiple_of(a * tq, tq)
        qts = [qa_ref[0, h, pl.ds(q0, tq), :] for h in range(nh)]

        def scores(j, buf):
            k0 = pl.multiple_of(j * tk, tk)
            for h in range(nh):
                buf[h] = _dot_nt(ka_ref[0, h, pl.ds(k0, tk), :], qts[h])

        def update(j, buf, states, mask_off):
            out = []
            for h in range(nh):
                m, acc = states[h]
                st = buf[h]
                if mask_off is not None:
                    st = jnp.where(krow + mask_off <= qcol, st, NEG_BIG)
                m_new = jnp.maximum(m, jnp.max(st, axis=0, keepdims=True))
                alpha = jnp.exp2(m - m_new)
                p = jnp.exp2(st - m_new).astype(BF16)
                vt = jnp.concatenate(
                    [vt_ref[0, j, h * HEAD_DIM:(h + 1) * HEAD_DIM, :], ones_ref[...]], axis=0)
                out.append((m_new, alpha * acc + _dot(vt, p)))
            return tuple(out)

        def two_steps(i, states):
            scores(2 * i + 1, st1)
            states = update(2 * i, st0, states, None)
            scores(2 * i + 2, st0)
            return update(2 * i + 1, st1, states, None)

        init = tuple((jnp.full((1, tq), NEG_BIG, F32), jnp.zeros((HEAD_DIM + ONES_ROWS, tq), F32))
                     for _ in range(nh))
        scores(0, st0)
        states = lax.fori_loop(0, a, two_steps, init)
        scores(2 * a + 1, st1)
        states = update(2 * a, st0, states, 0)
        states = update(2 * a + 1, st1, states, tk)
        ot = jnp.concatenate(
            [acc[0:HEAD_DIM] / acc[HEAD_DIM:HEAD_DIM + 1] for (_, acc) in states], axis=0)
        o_ref[0, pl.ds(q0, tq), :] = ot.T.astype(o_ref.dtype)
        return carry

    lax.fori_loop(0, t // tq, q_tile, 0)


def _attention_t(qa, ka, vt, *, heads_per_step):
    bsz, nh, t, w = qa.shape
    nt, e, tk = vt.shape[1], vt.shape[2], vt.shape[3]
    g = heads_per_step
    tq = 2 * tk
    ones = (jnp.arange(ONES_ROWS)[:, None] == 0).astype(BF16) * jnp.ones((1, tk), BF16)
    return pl.pallas_call(
        functools.partial(_attn_t_body, tq=tq),
        grid=(bsz, nh // g),
        in_specs=[
            pl.BlockSpec((1, g, t, w), lambda b, h: (b, h, 0, 0)),
            pl.BlockSpec((1, g, t, w), lambda b, h: (b, h, 0, 0)),
            pl.BlockSpec((1, nt, g * HEAD_DIM, tk), lambda b, h: (b, 0, h, 0)),
            _const_spec((ONES_ROWS, tk)),
        ],
        out_specs=pl.BlockSpec((1, t, g * HEAD_DIM), lambda b, h: (b, 0, h)),
        out_shape=jax.ShapeDtypeStruct((bsz, t, e), BF16),
        scratch_shapes=[pltpu.VMEM((g, tk, tq), F32), pltpu.VMEM((g, tk, tq), F32)],
        compiler_params=pltpu.CompilerParams(
            dimension_semantics=("arbitrary", "arbitrary"), vmem_limit_bytes=VMEM_LIMIT),
        name="attn_prompt",
    )(qa, ka, vt, ones)


def _suffix_body(lft_ref, tri_ref, r_ref, carry):
    j = pl.program_id(0)

    @pl.when(j == 0)
    def _():
        carry[...] = jnp.zeros(carry.shape, F32)

    lft = lft_ref[...]
    n = lft.shape[0]
    c3 = _dot(jnp.concatenate(_split3(lft), axis=0), tri_ref[...])
    incl = c3[0:n] + c3[n:2 * n] + c3[2 * n:3 * n] + carry[:, 0:1]
    r_ref[...] = incl - lft
    carry[...] = jnp.broadcast_to(incl[:, 0:1], carry.shape)


def _suffix_sums(lft, *, tk):
    n, p = lft.shape
    nkb = p // tk
    a = jnp.arange(tk)
    tri = (a[:, None] >= a[None, :]).astype(BF16)
    return pl.pallas_call(
        _suffix_body,
        grid=(nkb,),
        in_specs=[pl.BlockSpec((n, tk), lambda j: (0, nkb - 1 - j)), _const_spec((tk, tk))],
        out_specs=pl.BlockSpec((n, tk), lambda j: (0, nkb - 1 - j)),
        out_shape=jax.ShapeDtypeStruct((n, p), F32),
        scratch_shapes=[pltpu.VMEM((n, 128), F32)],
        compiler_params=pltpu.CompilerParams(
            dimension_semantics=("arbitrary",), vmem_limit_bytes=VMEM_LIMIT),
        name="suffix_sums",
    )(lft, tri)


def _attn_sample_body(q_ref, knt_ref, vnt_ref, cn_ref, kct_ref, vct_ref, r_ref, o_ref):
    g, nq = q_ref.shape[1], q_ref.shape[2]
    npad = knt_ref.shape[3]
    row = lax.broadcasted_iota(jnp.int32, (nq, npad), 0)
    col = lax.broadcasted_iota(jnp.int32, (nq, npad), 1)
    for h in range(g):
        q = q_ref[0, h]
        s_old = _dot(q, kct_ref[0, h].astype(BF16)) + r_ref[0, h:h + 1, :] * LOG2E
        s_new = _dot(q, knt_ref[0, h]) - cn_ref[0, h:h + 1, :] * LOG2E
        s_new = jnp.where(col <= row, s_new, NEG_BIG)
        m = jnp.maximum(jnp.max(s_old, axis=-1, keepdims=True), jnp.max(s_new, axis=-1, keepdims=True))
        p_old = jnp.exp2(s_old - m)
        p_new = jnp.exp2(s_new - m)
        l = jnp.sum(p_old, axis=-1, keepdims=True) + jnp.sum(p_new, axis=-1, keepdims=True)
        o = (_dot_nt(p_old.astype(BF16), vct_ref[0, h].astype(BF16))
             + _dot_nt(p_new.astype(BF16), vnt_ref[0, h]))
        o_ref[0, h] = o / l


def _attention_sample(q, knt, vnt, cn, kct, vct, r, *, heads_per_step):
    bsz, nh, nq, dh = q.shape
    npad, p = knt.shape[3], kct.shape[3]
    g = heads_per_step
    per_head = lambda *shape: pl.BlockSpec((1, g) + shape, lambda b, h: (b, h) + (0,) * len(shape))
    per_group = lambda n: pl.BlockSpec((1, g, n), lambda b, h: (b * (nh // g) + h, 0, 0))
    return pl.pallas_call(
        _attn_sample_body,
        grid=(bsz, nh // g),
        in_specs=[
            per_head(nq, dh), per_head(dh, npad), per_head(dh, npad), per_group(npad),
            per_head(dh, p), per_head(dh, p), per_group(p),
        ],
        out_specs=per_head(nq, dh),
        out_shape=jax.ShapeDtypeStruct((bsz, nh, nq, dh), F32),
        compiler_params=pltpu.CompilerParams(
            dimension_semantics=("arbitrary", "arbitrary"), vmem_limit_bytes=VMEM_LIMIT),
        name="attn_sample",
    )(q, knt, vnt, cn.reshape(bsz * nh // g, g, npad), kct, vct, r.reshape(bsz * nh // g, g, p))


def _fox_out_body(o_ref, z_ref, x_ref, w_ref, g_ref, y_ref):
    gated = (o_ref[...].astype(F32) * jax.nn.silu(z_ref[...])).astype(BF16)
    y = x_ref[...] + _dot(gated, w_ref[...])
    y_ref[...] = _rms(y) * _rows(g_ref, y.shape[0])


def _fox_out(o, z, x, w_out, final_g, *, tq):
    n, e = o.shape
    d = w_out.shape[1]
    row = lambda i: (i, 0)
    return pl.pallas_call(
        _fox_out_body,
        grid=(n // tq,),
        in_specs=[
            pl.BlockSpec((tq, e), row), pl.BlockSpec((tq, e), row), pl.BlockSpec((tq, d), row),
            _const_spec((e, d)), _const_spec((SUBLANES, d)),
        ],
        out_specs=pl.BlockSpec((tq, d), row),
        out_shape=jax.ShapeDtypeStruct((n, d), F32),
        compiler_params=pltpu.CompilerParams(
            dimension_semantics=("arbitrary",), vmem_limit_bytes=VMEM_LIMIT),
        name="fox_out",
    )(o, z, x, w_out, final_g)


def _upper_tri(n, seg):
    a = jnp.arange(n)[:, None]
    b = jnp.arange(n)[None, :]
    return ((a <= b) & (a // seg == b // seg)).astype(BF16)


def _tile_rows(n, pref):
    t = min(n, pref)
    assert n % t == 0
    return t


def kernel(x_prompt, x_sample, state_conv, cache_k, cache_v, cache_logf, norm_g, final_norm_g,
           w_conv_in, w_dw, b_dw, conv_ln_g, conv_ln_b, w_conv_out,
           w_fox_in, b_forget, q_norm_g, k_norm_g, w_fox_out):
    bp, tp, d = x_prompt.shape
    bs, ts, _ = x_sample.shape
    past = cache_k.shape[2]
    e = N_HEADS * HEAD_DIM
    assert state_conv.shape[0] == 1 and cache_k.shape[0] == 1 and norm_g.shape[0] == 2

    w_dw8 = jnp.broadcast_to(w_dw[0][:, None, :], (CONV_WIDTH, 8, e))
    conv_w = (_spread(norm_g[0]), w_conv_in[0].astype(BF16), w_dw8, _spread(b_dw[0]),
              _spread(conv_ln_g[0]), _spread(conv_ln_b[0]), w_conv_out[0].astype(BF16))
    zero_state = jnp.zeros((bp, CONV_STATE, e), F32)
    y1p, conv_p = _conv_layer(x_prompt, zero_state, *conv_w, bb=1, tq=_tile_rows(tp, 256))
    y1s, conv_s = _conv_layer(x_sample, state_conv[0], *conv_w, bb=_tile_rows(bs, 8), tq=ts)

    wf = w_fox_in[0]
    wq, wk, wv, wz = (wf[:, i * e:(i + 1) * e].astype(BF16) for i in range(4))
    wfl_cols = wf[:, 4 * e:]
    wfl = jnp.pad(wfl_cols, ((0, 0), (0, 128 - N_HEADS))).astype(BF16)
    wflt = wfl_cols.T.astype(BF16)
    bfr = _spread(jnp.pad(b_forget[0], (0, 128 - N_HEADS)))
    bfc = b_forget[0][:, None]
    qg = _spread(jnp.tile(q_norm_g[0], N_HEADS) * (LOG2E / math.sqrt(HEAD_DIM)))
    kg = _spread(jnp.tile(k_norm_g[0], N_HEADS))
    w = 2 * HEAD_DIM
    head_of = jnp.arange(e) // HEAD_DIM
    lanes = jnp.arange(w)
    hsum = ((head_of[:, None] == lanes[None, :]).astype(F32) / HEAD_DIM).astype(BF16)
    hexp = (jnp.arange(2 * w)[:, None] % w == head_of[None, :]).astype(BF16)
    fox_w = (_spread(norm_g[1]), wq, wk, wv, wz, wfl, wflt, bfr, bfc, qg, kg, hsum, hexp)

    tqp = _tile_rows(tp, 256)
    rr = jnp.arange(tqp)
    ltri = (rr[None, :] <= rr[:, None]).astype(BF16)
    term, head = lanes // N_HEADS, lanes % N_HEADS
    dst = jnp.arange(N_HEADS * w)
    place = ((term[:, None] < AUG)
             & (dst[None, :] == head[:, None] * w + HEAD_DIM + term[:, None])).astype(BF16)
    qap, kap, kfp, vfp, vtp, zp, lfp = _fox_in_prompt(
        y1p, _spread(norm_g[1]), wq, wk, wv, wz, wfl, bfr, qg, kg, hsum, hexp, ltri, place, tq=tqp)
    ns = bs * ts
    qs, kfs, kbs, vfs, vbs, zs, lfs, cts = _fox_in(
        y1s.reshape(ns, d), *fox_w, _upper_tri(ns, ts), tq=ns, tiles_per_seg=1)

    op = _attention_t(qap, kap, vtp, heads_per_step=4)

    heads_first = lambda x: x.reshape(bs, ts, N_HEADS, HEAD_DIM).transpose(0, 2, 1, 3)
    time_last = lambda x: jnp.pad(x.reshape(bs, ts, N_HEADS, HEAD_DIM).transpose(0, 2, 3, 1),
                                  ((0, 0), (0, 0), (0, 0), (0, 128 - ts)))
    cn = jnp.pad(cts.reshape(N_HEADS, bs, ts).transpose(1, 0, 2), ((0, 0), (0, 0), (0, 128 - ts)))
    r_past = _suffix_sums(jnp.swapaxes(cache_logf[0], 1, 2).reshape(bs * N_HEADS, past),
                          tk=_tile_rows(past, 512))
    os_ = _attention_sample(
        heads_first(qs), time_last(kbs), time_last(vbs), cn,
        cache_k[0].transpose(0, 2, 3, 1), cache_v[0].transpose(0, 2, 3, 1), r_past,
        heads_per_step=4)
    os_ = os_.transpose(0, 2, 1, 3)

    wo = w_fox_out[0].astype(BF16)
    fg = _spread(final_norm_g)
    yp = _fox_out(op.reshape(bp * tp, e), zp.reshape(bp * tp, e), y1p.reshape(bp * tp, d), wo, fg,
                  tq=_tile_rows(bp * tp, 512))
    ys = _fox_out(os_.reshape(ns, e), zs, y1s.reshape(ns, d), wo, fg, tq=ns)

    return (yp.reshape(bp, tp, d), ys.reshape(bs, ts, d),
            conv_p[None], conv_s[None],
            kfp.reshape(1, bp, tp, N_HEADS, HEAD_DIM), vfp.reshape(1, bp, tp, N_HEADS, HEAD_DIM),
            lfp.reshape(1, bp, tp, N_HEADS),
            kfs.reshape(1, bs, ts, N_HEADS, HEAD_DIM), vfs.reshape(1, bs, ts, N_HEADS, HEAD_DIM),
            lfs.reshape(1, bs, ts, N_HEADS))
```
